```python
import math
import jax
import jax.numpy as jnp
from jax import lax
import numpy as np

D_MODEL = 4096
BATCH = 2
SEQ = 8192
DEPTH = 4

GRID_W = 64
CTX_LEN = 256
CHUNK = 64
EPS = 1e-6
N_MOD = 6

SSD_WIDTH = D_MODEL // 2
SSD_HEAD_DIM = 64
SSD_HEADS = SSD_WIDTH // SSD_HEAD_DIM
SSD_GROUPS = 4
SSD_HPG = SSD_HEADS // SSD_GROUPS
SSD_STATE = 128
SSD_CONV = 5
SSD_CONV_CH = SSD_WIDTH + 2 * SSD_GROUPS * SSD_STATE

HG_WIDTH = D_MODEL // 2
HG_K = 128
HG_V = 128
HG_HEADS = HG_WIDTH // HG_K

D_MIX = SSD_WIDTH + HG_WIDTH
SPLITS = [SSD_WIDTH,
          SSD_WIDTH + SSD_CONV_CH,
          SSD_WIDTH + SSD_CONV_CH + 2 * SSD_HEADS,
          SSD_WIDTH + SSD_CONV_CH + 2 * SSD_HEADS + HG_WIDTH,
          SSD_WIDTH + SSD_CONV_CH + 2 * SSD_HEADS + 3 * HG_WIDTH,
          SSD_WIDTH + SSD_CONV_CH + 2 * SSD_HEADS + 4 * HG_WIDTH]
IN_COLS = SSD_WIDTH + SSD_CONV_CH + 2 * SSD_HEADS + 5 * HG_WIDTH

MOE_GROUPS = 4
MOE_EPG = 4
N_EXPERTS = MOE_GROUPS * MOE_EPG
TOP_K = 2
D_EXPERT = D_MODEL // 16

kernel_name = 'hymba_ssd_hgrn2_shared_router_moe_dit'


def _rms(x):
    xf = x.astype(jnp.float32)
    return xf * lax.rsqrt(jnp.mean(jnp.square(xf), axis=-1, keepdims=True) + EPS)


def rmsnorm(x, w):
    return (_rms(x) * w.astype(jnp.float32)).astype(x.dtype)


def modulate(h, shift, scale):
    return h * (1.0 + scale[:, None]) + shift[:, None]


def depthwise_conv(x, w, b):
    pad = (SSD_CONV - 1) // 2
    y = lax.conv_general_dilated(x, w[:, None, :].astype(x.dtype), window_strides=(1,),
                                 padding=[(pad, pad)], dimension_numbers=('NWC', 'WIO', 'NWC'),
                                 feature_group_count=x.shape[-1])
    return y + b.astype(x.dtype)


def segsum(a):
    t = a.shape[-1]
    ae = jnp.broadcast_to(a[..., :, None], a.shape + (t,))
    strict = jnp.tril(jnp.ones((t, t), dtype=bool), -1)
    cs = jnp.cumsum(jnp.where(strict, ae, 0.0), axis=-2)
    return jnp.where(jnp.tril(jnp.ones((t, t), dtype=bool)), cs, -jnp.inf)


def ssd_scan(x, dt, a_log, bm, init, cm=None):
    bsz, seq = x.shape[:2]
    nc = seq // CHUNK
    chunks = lambda t: t.reshape((bsz, nc, CHUNK) + t.shape[2:])
    xs = chunks(x * dt[..., None])
    bc = chunks(bm)
    la = jnp.moveaxis(chunks(dt * -jnp.exp(a_log)), (3, 4), (1, 2))
    cum = jnp.cumsum(la, axis=-1)
    chunk_states = jnp.einsum('bcsgn,bgecs,bcsgep->bcgepn', bc, jnp.exp(cum[..., -1:] - cum), xs)
    states = jnp.concatenate([init[:, None], chunk_states], axis=1)
    chunk_decay = jnp.pad(cum[..., -1], ((0, 0), (0, 0), (0, 0), (1, 0)))
    states = jnp.einsum('bgezc,bcgepn->bzgepn', jnp.exp(segsum(chunk_decay)), states)
    final = states[:, -1]
    if cm is None:
        return None, final
    cc = chunks(cm)
    cb = jnp.einsum('bclgn,bcsgn->bgcls', cc, bc)
    y_diag = jnp.einsum('bgecls,bcsgep->bclgep', cb[:, :, None] * jnp.exp(segsum(la)), xs)
    y_off = jnp.einsum('bclgn,bcgepn,bgecl->bclgep', cc, states[:, :-1], jnp.exp(cum))
    return (y_diag + y_off).reshape(x.shape), final


def gla_scan(k, v, log_f, init, q=None):
    bsz, seq, heads, _ = k.shape
    nc = seq // CHUNK
    blocks = lambda t: t.reshape(bsz, nc, CHUNK, heads, t.shape[-1]).transpose(1, 0, 3, 2, 4)
    mask = jnp.tril(jnp.ones((CHUNK, CHUNK), dtype=bool))[:, :, None]

    def step(state, blk):
        kc, vc, gc = blk[0], blk[1], blk[2]
        cum = jnp.cumsum(gc, axis=-2)
        last = cum[..., -1:, :]
        new_state = (jnp.exp(last)[..., 0, :, None] * state
                     + jnp.einsum('bhjk,bhjv->bhkv', kc * jnp.exp(last - cum), vc))
        if q is None:
            return new_state, None
        qc = blk[3]
        rel = jnp.exp(jnp.where(mask, cum[..., :, None, :] - cum[..., None, :, :], -jnp.inf))
        scores = jnp.einsum('bhik,bhjk,bhijk->bhij', qc, kc, rel)
        out = (jnp.einsum('bhij,bhjv->bhiv', scores, vc)
               + jnp.einsum('bhik,bhkv->bhiv', qc * jnp.exp(cum), state))
        return new_state, out

    xs = (blocks(k), blocks(v), blocks(log_f)) + (() if q is None else (blocks(q),))
    final, out = lax.scan(step, init, xs)
    if q is None:
        return None, final
    return out.transpose(1, 0, 3, 2, 4).reshape(bsz, seq, heads, -1), final


def to_column_major(t, rows):
    b, n, ch = t.shape
    return t.reshape(b, rows, GRID_W, ch).transpose(0, 2, 1, 3).reshape(b, n, ch)


def from_column_major(t, rows):
    b, n, ch = t.shape
    return t.reshape(b, GRID_W, rows, ch).transpose(0, 2, 1, 3).reshape(b, n, ch)


def ssd_mixer(z, xbc, dt_raw, lp, init, emit):
    f32 = jnp.float32
    bsz, seq, _ = xbc.shape
    xbc = jax.nn.silu(depthwise_conv(xbc, lp['conv_w'], lp['conv_b'])).astype(f32)
    xs, bm, cm = jnp.split(xbc, [SSD_WIDTH, SSD_WIDTH + SSD_GROUPS * SSD_STATE], axis=-1)
    xs = xs.reshape(bsz, seq, SSD_GROUPS, SSD_HPG, SSD_HEAD_DIM)
    bm = bm.reshape(bsz, seq, SSD_GROUPS, SSD_STATE)
    cm = cm.reshape(bsz, seq, SSD_GROUPS, SSD_STATE)
    dt = jax.nn.softplus(dt_raw.astype(f32) + lp['dt_bias'].reshape(-1).astype(f32))
    dt = dt.reshape(bsz, seq, 2, SSD_GROUPS, SSD_HPG)
    a_log = lp['a_log'].astype(f32).reshape(2, SSD_GROUPS, SSD_HPG)
    flip = lambda t: jnp.flip(t, axis=1)
    y_f, s_f = ssd_scan(xs, dt[:, :, 0], a_log[0], bm, init[0], cm if emit else None)
    y_b, s_b = ssd_scan(flip(xs), flip(dt[:, :, 1]), a_log[1], flip(bm), init[1],
                        flip(cm) if emit else None)
    if not emit:
        return None, (s_f, s_b)
    y = y_f + flip(y_b) + lp['d'].astype(f32).reshape(SSD_GROUPS, SSD_HPG)[..., None] * xs
    y = y.reshape(bsz, seq, SSD_WIDTH) * jax.nn.silu(z.astype(f32))
    y = _rms(y.reshape(bsz, seq, SSD_GROUPS, -1)).reshape(bsz, seq, SSD_WIDTH)
    return (y * lp['ssd_norm'].astype(f32)).astype(z.dtype), (s_f, s_b)


def hgrn_mixer(q, f_raw, v_in, g, lb, norm_w, init, rows, emit):
    f32 = jnp.float32
    bsz, seq, _ = f_raw.shape
    order = (lambda t: to_column_major(t, rows)) if rows else (lambda t: t)
    fr = order(f_raw).astype(f32).reshape(bsz, seq, 2, HG_HEADS, HG_K)
    lbh = lb.astype(f32).reshape(2, HG_HEADS, HG_K)
    log_f = jnp.logaddexp(jnp.log(lbh), jnp.log1p(-lbh) + jax.nn.log_sigmoid(fr))
    k = -jnp.expm1(log_f)
    v = order(v_in).astype(f32).reshape(bsz, seq, HG_HEADS, HG_V)
    qh = jax.nn.silu(order(q).astype(f32)).reshape(bsz, seq, HG_HEADS, HG_K) if emit else None
    flip = lambda t: jnp.flip(t, axis=1)
    o_f, s_f = gla_scan(k[:, :, 0], v, log_f[:, :, 0], init[0], qh)
    o_b, s_b = gla_scan(flip(k[:, :, 1]), flip(v), flip(log_f[:, :, 1]), init[1],
                        flip(qh) if emit else None)
    if not emit:
        return None, (s_f, s_b)
    o = (_rms(o_f + flip(o_b)) * norm_w.astype(f32)).reshape(bsz, seq, HG_WIDTH)
    if rows:
        o = from_column_major(o, rows)
    return (o * jax.nn.silu(g.astype(f32))).astype(g.dtype), (s_f, s_b)


def token_mix(h, w_in, lp, ssd_init, hg_init, lb, rows, emit):
    z, xbc, dt_raw, q, f_raw, v_in, g = jnp.split(h @ w_in, SPLITS, axis=-1)
    y_ssd, ssd_fin = ssd_mixer(z, xbc, dt_raw, lp, ssd_init, emit)
    y_hg, hg_fin = hgrn_mixer(q, f_raw, v_in, g, lb, lp['hg_norm'], hg_init, rows, emit)
    y = jnp.concatenate([y_ssd, y_hg], axis=-1) if emit else None
    return y, ssd_fin, hg_fin


def moe_ffn(h, router_w, w_gate, w_up, w_down):
    probs = jax.nn.softmax((h @ router_w).astype(jnp.float32), axis=-1)
    grouped = probs.reshape(probs.shape[:-1] + (MOE_GROUPS, MOE_EPG))
    group_score = lax.top_k(grouped, TOP_K)[0].sum(-1)
    best = jnp.argmax(group_score, axis=-1)
    in_group = (best[..., None] == jnp.arange(MOE_GROUPS))[..., None]
    cand = jnp.where(in_group, grouped, -1.0).reshape(probs.shape)
    vals, idx = lax.top_k(cand, TOP_K)
    w = vals / jnp.sum(vals, axis=-1, keepdims=True)
    gates = jnp.einsum('blk,blke->ble', w, jax.nn.one_hot(idx, N_EXPERTS, dtype=jnp.float32))
    a = jnp.einsum('bld,edf->blef', h, w_gate)
    u = jnp.einsum('bld,edf->blef', h, w_up)
    act = jax.nn.silu(a) * u * gates[..., None].astype(a.dtype)
    return jnp.einsum('blef,efd->bld', act, w_down)


def setup_inputs(seed: int = 0) -> dict:
    key = jax.random.key(seed)
    ks = jax.random.split(key, 24)
    f32 = jnp.float32

    def normal(k, shape, scale):
        return jax.random.normal(k, shape, f32) * scale

    def gain(k, shape):
        return 1.0 + normal(k, shape, 0.02)

    dt0 = jnp.exp(jax.random.uniform(ks[9], (DEPTH, 2, SSD_HEADS), f32,
                                     math.log(1e-3), math.log(1e-1)))
    return {
        'x': normal(ks[0], (BATCH, SEQ, D_MODEL), 1.0),
        'c': normal(ks[1], (BATCH, D_MODEL), 1.0),
        'ctx': normal(ks[2], (BATCH, CTX_LEN, D_MODEL), 1.0),
        'c_ctx': normal(ks[3], (D_MODEL,), 1.0),
        'ada_w': normal(ks[4], (DEPTH, D_MODEL, N_MOD * D_MODEL), 0.5 * D_MODEL ** -0.5),
        'ada_b': normal(ks[5], (DEPTH, N_MOD * D_MODEL), 0.02),
        'norm_mix_w': gain(ks[6], (DEPTH, D_MODEL)),
        'norm_ffn_w': gain(ks[7], (DEPTH, D_MODEL)),
        'w_in': normal(ks[8], (DEPTH, D_MODEL, IN_COLS), D_MODEL ** -0.5),
        'ssd_conv_w': normal(ks[10], (DEPTH, SSD_CONV, SSD_CONV_CH), SSD_CONV ** -0.5),
        'ssd_conv_b': normal(ks[11], (DEPTH, SSD_CONV_CH), 0.02),
        'ssd_dt_bias': dt0 + jnp.log(-jnp.expm1(-dt0)),
        'ssd_a_log': jnp.log(jax.random.uniform(ks[12], (DEPTH, 2, SSD_HEADS), f32, 1.0, 16.0)),
        'ssd_d': gain(ks[13], (DEPTH, SSD_HEADS)),
        'ssd_norm_w': gain(ks[14], (DEPTH, SSD_WIDTH)),
        'hgrn_lb_logits': normal(ks[15], (DEPTH, 2, HG_WIDTH), 0.1),
        'hgrn_norm_w': gain(ks[16], (DEPTH, HG_V)),
        'w_out': normal(ks[17], (DEPTH, D_MIX, D_MODEL), D_MIX ** -0.5),
        'router_w': normal(ks[18], (D_MODEL, N_EXPERTS), D_MODEL ** -0.5),
        'moe_w_gate': normal(ks[19], (DEPTH, N_EXPERTS, D_MODEL, D_EXPERT), D_MODEL ** -0.5),
        'moe_w_up': normal(ks[20], (DEPTH, N_EXPERTS, D_MODEL, D_EXPERT), D_MODEL ** -0.5),
        'moe_w_down': normal(ks[21], (DEPTH, N_EXPERTS, D_EXPERT, D_MODEL), D_EXPERT ** -0.5),
        'final_norm_w': gain(ks[22], (D_MODEL,)),
    }


def reference(x, c, ctx, c_ctx, ada_w, ada_b, norm_mix_w, norm_ffn_w, w_in, ssd_conv_w,
              ssd_conv_b, ssd_dt_bias, ssd_a_log, ssd_d, ssd_norm_w, hgrn_lb_logits,
              hgrn_norm_w, w_out, router_w, moe_w_gate, moe_w_up, moe_w_down, final_norm_w):
    f32 = jnp.float32
    bsz, seq, _ = x.shape
    rows = seq // GRID_W
    lb_p = jax.nn.softmax(hgrn_lb_logits.astype(f32), axis=0)
    lower_bounds = jnp.cumsum(lb_p, axis=0) - lb_p[0]
    silu_c = jax.nn.silu(c)
    silu_cc = jax.nn.silu(c_ctx)[None]
    zero_ssd = jnp.zeros((bsz, SSD_GROUPS, SSD_HPG, SSD_HEAD_DIM, SSD_STATE), f32)
    zero_hg = jnp.zeros((bsz, HG_HEADS, HG_K, HG_V), f32)
    for l in range(DEPTH):
        ctx_continues = l < DEPTH - 1
        lp = {'conv_w': ssd_conv_w[l], 'conv_b': ssd_conv_b[l], 'dt_bias': ssd_dt_bias[l],
              'a_log': ssd_a_log[l], 'd': ssd_d[l], 'ssd_norm': ssd_norm_w[l],
              'hg_norm': hgrn_norm_w[l]}
        m_lat = jnp.split(silu_c @ ada_w[l] + ada_b[l], N_MOD, axis=-1)
        m_ctx = jnp.split(silu_cc @ ada_w[l] + ada_b[l], N_MOD, axis=-1)
        h_ctx = modulate(rmsnorm(ctx, norm_mix_w[l]), m_ctx[0], m_ctx[1])
        y_ctx, ssd_st, hg_st = token_mix(h_ctx, w_in[l], lp, (zero_ssd, zero_ssd),
                                         (zero_hg, zero_hg), lower_bounds[l], None, ctx_continues)
        h_lat = modulate(rmsnorm(x, norm_mix_w[l]), m_lat[0], m_lat[1])
        y_lat, _, _ = token_mix(h_lat, w_in[l], lp, ssd_st, hg_st, lower_bounds[l], rows, True)
        x = x + m_lat[2][:, None] * (y_lat @ w_out[l])
        h_lat = modulate(rmsnorm(x, norm_ffn_w[l]), m_lat[3], m_lat[4])
        x = x + m_lat[5][:, None] * moe_ffn(h_lat, router_w, moe_w_gate[l], moe_w_up[l], moe_w_down[l])
        if ctx_continues:
            ctx = ctx + m_ctx[2][:, None] * (y_ctx @ w_out[l])
            h_ctx = modulate(rmsnorm(ctx, norm_ffn_w[l]), m_ctx[3], m_ctx[4])
            ctx = ctx + m_ctx[5][:, None] * moe_ffn(h_ctx, router_w, moe_w_gate[l], moe_w_up[l],
                                                    moe_w_down[l])
    return rmsnorm(x, final_norm_w)
```

```python
import functools
import math

import numpy as np
import jax
import jax.numpy as jnp
from jax import lax
from jax.experimental import pallas as pl
from jax.experimental.pallas import tpu as pltpu

F32 = jnp.float32
BF16 = jnp.bfloat16

CHUNK = 64
GRID_W = 64
EPS = 1e-6
N_MOD = 6
SSD_HEAD_DIM = 64
SSD_GROUPS = 4
SSD_STATE = 128
SSD_CONV = 5
HG_K = 128
HG_V = 128
MOE_GROUPS = 4
MOE_EPG = 4
LANE = 128
SUBLANE = 8
VMEM_LIMIT = 56 * 1024 * 1024
NEG_BIG = -1e30


def _cparams(n_axes):
    return pltpu.CompilerParams(dimension_semantics=("arbitrary",) * n_axes,
                                vmem_limit_bytes=VMEM_LIMIT)


def _silu(x):
    return x * jax.nn.sigmoid(x)


def _split_bf16(x):
    hi = x.astype(BF16)
    lo = (x - hi.astype(F32)).astype(BF16)
    return hi, lo


def _dot(a, b):
    return jnp.dot(a, b, preferred_element_type=F32)


def _dot_nt(a, b):
    return lax.dot_general(a, b, (((1,), (1,)), ((), ())), preferred_element_type=F32)


def _dot_tn(a, b):
    return lax.dot_general(a, b, (((0,), (0,)), ((), ())), preferred_element_type=F32)


def _dot_exact_lhs(a_bf, x):
    hi, lo = _split_bf16(x)
    return _dot(a_bf, hi) + _dot(a_bf, lo)


def _dot_exact_rhs(x, b_bf):
    hi, lo = _split_bf16(x)
    return _dot(hi, b_bf) + _dot(lo, b_bf)


def _ada_kernel(c_ref, w_ref, b_ref, o_ref):
    s = _silu(c_ref[...]).astype(BF16)
    o_ref[...] = _dot(s, w_ref[...].astype(BF16)) + b_ref[...]


def ada_modulation(cc, ada_w, ada_b, tn=512):
    depth, d, n = ada_w.shape
    rows = cc.shape[0]
    return pl.pallas_call(
        _ada_kernel,
        grid=(depth, n // tn),
        in_specs=[pl.BlockSpec((rows, d), lambda l, j: (0, 0)),
                  pl.BlockSpec((None, d, tn), lambda l, j: (l, 0, j)),
                  pl.BlockSpec((None, 1, tn), lambda l, j: (l, 0, j))],
        out_specs=pl.BlockSpec((None, rows, tn), lambda l, j: (l, 0, j)),
        out_shape=jax.ShapeDtypeStruct((depth, rows, n), F32),
        compiler_params=_cparams(2),
        name="ada_modulation",
    )(cc, ada_w, ada_b.reshape(depth, 1, n))


def _norm_mod(x_ref, nw_ref, sh_ref, sc_ref):
    x = x_ref[...]
    ms = jnp.mean(x * x, axis=-1, keepdims=True)
    h = x * lax.rsqrt(ms + EPS) * nw_ref[...]
    return h * (1.0 + sc_ref[...]) + sh_ref[...]


def _norm_mod_kernel(x_ref, nw_ref, sh_ref, sc_ref, o_ref):
    o_ref[...] = _norm_mod(x_ref, nw_ref, sh_ref, sc_ref).astype(BF16)


def _expert_gates(logits, n_exp):
    lane = lax.broadcasted_iota(jnp.int32, logits.shape, 1)
    lane_f = lane.astype(F32)
    valid = lane < n_exp
    lg = jnp.where(valid, logits, NEG_BIG)
    mx = jnp.max(lg, axis=-1, keepdims=True)
    e = jnp.where(valid, jnp.exp(lg - mx), 0.0)
    p = e / jnp.sum(e, axis=-1, keepdims=True)
    best = None
    for g in range(MOE_GROUPS):
        in_g = jnp.logical_and(lane >= g * MOE_EPG, lane < (g + 1) * MOE_EPG)
        pg = jnp.where(in_g, p, -1.0)
        m1 = jnp.max(pg, axis=-1, keepdims=True)
        i1 = jnp.min(jnp.where(pg == m1, lane_f, 1e6), axis=-1, keepdims=True)
        pg2 = jnp.where(lane_f == i1, -1.0, pg)
        m2 = jnp.max(pg2, axis=-1, keepdims=True)
        i2 = jnp.min(jnp.where(pg2 == m2, lane_f, 1e6), axis=-1, keepdims=True)
        cand = (m1 + m2, m1, i1, m2, i2)
        if best is None:
            best = cand
        else:
            better = cand[0] > best[0]
            best = tuple(jnp.where(better, a, b) for a, b in zip(cand, best))
    _, m1, i1, m2, i2 = best
    den = m1 + m2
    return (jnp.where(lane_f == i1, m1 / den, 0.0)
            + jnp.where(lane_f == i2, m2 / den, 0.0))


def _norm_mod_router_kernel(x_ref, nw_ref, sh_ref, sc_ref, rhi_ref, rlo_ref,
                            o_ref, g_ref, *, n_exp):
    h = _norm_mod(x_ref, nw_ref, sh_ref, sc_ref)
    hi, lo = _split_bf16(h)
    o_ref[...] = hi
    rhi = rhi_ref[...]
    logits = _dot(hi, rhi) + _dot(lo, rhi) + _dot(hi, rlo_ref[...])
    g_ref[...] = _expert_gates(logits, n_exp)


def _seg_fn(seq, n_batch, tm):
    per = seq // tm
    return lambda i: jnp.minimum(i // per, n_batch)


def norm_modulate(x_all, nw, mod, j_shift, j_scale, seq, n_batch, tm, router=None):
    t_all, d = x_all.shape
    seg = _seg_fn(seq, n_batch, tm)
    in_specs = [pl.BlockSpec((tm, d), lambda i: (i, 0)),
                pl.BlockSpec((1, d), lambda i: (0, 0)),
                pl.BlockSpec((None, 1, d), lambda i: (seg(i) * N_MOD + j_shift, 0, 0)),
                pl.BlockSpec((None, 1, d), lambda i: (seg(i) * N_MOD + j_scale, 0, 0))]
    args = [x_all, nw.reshape(1, d), mod, mod]
    if router is None:
        return pl.pallas_call(
            _norm_mod_kernel, grid=(t_all // tm,), in_specs=in_specs,
            out_specs=pl.BlockSpec((tm, d), lambda i: (i, 0)),
            out_shape=jax.ShapeDtypeStruct((t_all, d), BF16),
            compiler_params=_cparams(1), name="norm_modulate")(*args)
    rhi, rlo, n_exp = router
    in_specs += [pl.BlockSpec((d, LANE), lambda i: (0, 0))] * 2
    return pl.pallas_call(
        functools.partial(_norm_mod_router_kernel, n_exp=n_exp),
        grid=(t_all // tm,), in_specs=in_specs,
        out_specs=[pl.BlockSpec((tm, d), lambda i: (i, 0)),
                   pl.BlockSpec((tm, LANE), lambda i: (i, 0))],
        out_shape=[jax.ShapeDtypeStruct((t_all, d), BF16),
                   jax.ShapeDtypeStruct((t_all, LANE), F32)],
        compiler_params=_cparams(1), name="norm_modulate_router")(*args, rhi, rlo)


def _mm_kernel(a_ref, b_ref, o_ref):
    o_ref[...] = _dot(a_ref[...], b_ref[...]).astype(o_ref.dtype)


def matmul(a, b, out_dtype, tm, tn):
    t, k = a.shape
    n = b.shape[1]
    return pl.pallas_call(
        _mm_kernel, grid=(n // tn, t // tm),
        in_specs=[pl.BlockSpec((tm, k), lambda j, i: (i, 0)),
                  pl.BlockSpec((k, tn), lambda j, i: (0, j))],
        out_specs=pl.BlockSpec((tm, tn), lambda j, i: (i, j)),
        out_shape=jax.ShapeDtypeStruct((t, n), out_dtype),
        compiler_params=_cparams(2), name="matmul")(a, b)


def _glu_kernel(a_ref, wg_ref, wu_ref, g_ref, ex_ref, o_ref):
    a = a_ref[...]
    gate = _silu(_dot(a, wg_ref[...]))
    up = _dot(a, wu_ref[...])
    gexp = _dot_exact_rhs(g_ref[...], ex_ref[...])
    o_ref[...] = (gate * up * gexp).astype(o_ref.dtype)


def glu_matmul(a, wg, wu, gates, expand, tm, tn):
    t, k = a.shape
    n = wg.shape[1]
    return pl.pallas_call(
        _glu_kernel, grid=(n // tn, t // tm),
        in_specs=[pl.BlockSpec((tm, k), lambda j, i: (i, 0)),
                  pl.BlockSpec((k, tn), lambda j, i: (0, j)),
                  pl.BlockSpec((k, tn), lambda j, i: (0, j)),
                  pl.BlockSpec((tm, LANE), lambda j, i: (i, 0)),
                  pl.BlockSpec((LANE, tn), lambda j, i: (0, j))],
        out_specs=pl.BlockSpec((tm, tn), lambda j, i: (i, j)),
        out_shape=jax.ShapeDtypeStruct((t, n), BF16),
        compiler_params=_cparams(2), name="glu_matmul")(a, wg, wu, gates, expand)


def _mm_res_kernel(a1_ref, a2_ref, b1_ref, b2_ref, x_ref, g_ref, o_ref):
    y = _dot(a1_ref[...], b1_ref[...]) + _dot(a2_ref[...], b2_ref[...])
    o_ref[...] = x_ref[...] + g_ref[...] * y


def matmul_residual(a1, a1_blk, a2, a2_blk, w, x_all, mod, j_gate, seq, n_batch, tm, tn):
    t, d = x_all.shape
    k = w.shape[0]
    kh = k // 2
    seg = _seg_fn(seq, n_batch, tm)
    return pl.pallas_call(
        _mm_res_kernel, grid=(d // tn, t // tm),
        in_specs=[pl.BlockSpec((tm, kh), lambda j, i: (i, a1_blk)),
                  pl.BlockSpec((tm, kh), lambda j, i: (i, a2_blk)),
                  pl.BlockSpec((kh, tn), lambda j, i: (0, j)),
                  pl.BlockSpec((kh, tn), lambda j, i: (1, j)),
                  pl.BlockSpec((tm, tn), lambda j, i: (i, j)),
                  pl.BlockSpec((None, 1, tn), lambda j, i: (seg(i) * N_MOD + j_gate, 0, j))],
        out_specs=pl.BlockSpec((tm, tn), lambda j, i: (i, j)),
        out_shape=jax.ShapeDtypeStruct((t, d), F32),
        compiler_params=_cparams(2), name="matmul_residual")(a1, a2, w, w, x_all, mod)


def _conv_kernel(prev_ref, cur_ref, next_ref, w_ref, b_ref, o_ref, buf_ref, *,
                 tl, lat_tiles, tiles_per_seq, tiles_per_ctx):
    i = pl.program_id(0)
    r_lat = i % tiles_per_seq
    r_ctx = (i - lat_tiles) % tiles_per_ctx
    is_lat = i < lat_tiles
    first = jnp.where(is_lat, r_lat == 0, r_ctx == 0)
    last = jnp.where(is_lat, r_lat == tiles_per_seq - 1, r_ctx == tiles_per_ctx - 1)
    halo = SUBLANE
    buf_ref[0:halo, :] = jnp.where(first, 0.0, prev_ref[...].astype(F32))
    buf_ref[halo:halo + tl, :] = cur_ref[...].astype(F32)
    buf_ref[halo + tl:2 * halo + tl, :] = jnp.where(last, 0.0, next_ref[...].astype(F32))
    pad = (SSD_CONV - 1) // 2
    acc = jnp.broadcast_to(b_ref[...], (tl, b_ref.shape[1]))
    for j in range(SSD_CONV):
        off = halo - pad + j
        acc = acc + w_ref[j:j + 1, :] * buf_ref[off:off + tl, :]
    o_ref[...] = _silu(acc).astype(o_ref.dtype)


def conv_silu(p_ssd, col0, conv_w, conv_b, seq, ctx_len, t_lat, tl, cb):
    t_all = p_ssd.shape[0]
    ch = conv_w.shape[1]
    cblk0 = col0 // cb
    rb = tl // SUBLANE
    max_rb = t_all // SUBLANE - 1
    kern = functools.partial(_conv_kernel, tl=tl, lat_tiles=t_lat // tl,
                             tiles_per_seq=seq // tl, tiles_per_ctx=ctx_len // tl)
    return pl.pallas_call(
        kern, grid=(t_all // tl, ch // cb),
        in_specs=[pl.BlockSpec((SUBLANE, cb), lambda i, k: (jnp.maximum(i * rb - 1, 0), cblk0 + k)),
                  pl.BlockSpec((tl, cb), lambda i, k: (i, cblk0 + k)),
                  pl.BlockSpec((SUBLANE, cb), lambda i, k: (jnp.minimum((i + 1) * rb, max_rb), cblk0 + k)),
                  pl.BlockSpec((SSD_CONV, cb), lambda i, k: (0, k)),
                  pl.BlockSpec((1, cb), lambda i, k: (0, k))],
        out_specs=pl.BlockSpec((tl, cb), lambda i, k: (i, k)),
        out_shape=jax.ShapeDtypeStruct((t_all, ch), BF16),
        scratch_shapes=[pltpu.VMEM((tl + 2 * SUBLANE, cb), F32)],
        compiler_params=_cparams(2), name="conv_silu")(p_ssd, p_ssd, p_ssd, conv_w, conv_b.reshape(1, ch))


def _softplus(x):
    return jnp.maximum(x, 0.0) + jnp.log1p(jnp.exp(-jnp.abs(x)))


def _ssd_kernel(*refs, reverse, combine, nc, dcol):
    if combine:
        (xs_ref, bm_ref, cm_ref, dt_ref, bias_ref, nega_ref, ex_ref, tri_ref, init_ref,
         yf_ref, z_ref, dvec_ref, nw_ref, _, y_ref, fin_ref, st_ref) = refs
    else:
        (xs_ref, bm_ref, cm_ref, dt_ref, bias_ref, nega_ref, ex_ref, tri_ref, init_ref,
         y_ref, fin_ref, st_ref) = refs
    c = pl.program_id(1)

    @pl.when(c == 0)
    def _():
        st_ref[...] = init_ref[...]

    gw = st_ref.shape[2]
    n_groups = st_ref.shape[0]
    dt = _softplus(dt_ref[...] + bias_ref[...])
    la = dt * nega_ref[...]
    cum = _dot_exact_lhs(tri_ref[...], la)
    ex = ex_ref[...]
    dt_x = _dot_exact_rhs(dt, ex)
    cum_x = _dot_exact_rhs(cum, ex)
    edge = 0 if reverse else CHUNK - 1
    tot_x = cum_x[edge:edge + 1, :]
    exp_c = jnp.exp(cum_x)
    exp_t = jnp.exp(tot_x)
    cum_t = cum.T
    xs = xs_ref[...].astype(F32)
    xdt = xs * dt_x
    xdec = (xdt * jnp.exp(tot_x - cum_x)).astype(BF16)
    xdt_b = xdt.astype(BF16)
    row = lax.broadcasted_iota(jnp.int32, (CHUNK, 2 * SSD_HEAD_DIM), 0)
    lane = lax.broadcasted_iota(jnp.int32, (CHUNK, 2 * SSD_HEAD_DIM), 1)
    src = lane % SSD_HEAD_DIM
    causal = (row <= src) if reverse else (row >= src)
    lane2 = lax.broadcasted_iota(jnp.int32, (CHUNK, 2 * SSD_HEAD_DIM), 1)
    pairs = gw // (2 * SSD_HEAD_DIM)
    for g in range(n_groups):
        gs = slice(g * gw, (g + 1) * gw)
        ns = slice(g * SSD_STATE, (g + 1) * SSD_STATE)
        bm = bm_ref[:, ns]
        cm = cm_ref[:, ns]
        cb = _dot_nt(cm, bm)
        cb2 = jnp.concatenate([cb, cb], axis=1)
        st = st_ref[g]
        y_off = _dot(cm, st.astype(BF16)) * exp_c[:, gs]
        st_ref[g] = st * exp_t[:, gs] + _dot_tn(bm, xdec[:, gs])
        ys = []
        for j in range(pairs):
            c0 = g * gw + j * 2 * SSD_HEAD_DIM
            head = dcol + c0 // SSD_HEAD_DIM
            col_pair = cum_x[:, c0:c0 + 2 * SSD_HEAD_DIM]
            row_pair = jnp.concatenate([cum_t[head:head + 1, :], cum_t[head + 1:head + 2, :]], axis=1)
            decay = jnp.exp(jnp.where(causal, col_pair - row_pair, NEG_BIG))
            m = (cb2 * decay).astype(BF16)
            xp = xdt_b[:, c0:c0 + 2 * SSD_HEAD_DIM]
            zero = jnp.zeros_like(xp)
            rhs = jnp.concatenate([jnp.where(lane2 < SSD_HEAD_DIM, xp, zero),
                                   jnp.where(lane2 >= SSD_HEAD_DIM, xp, zero)], axis=0)
            ys.append(_dot(m, rhs))
        y = jnp.concatenate(ys, axis=1) + y_off
        if combine:
            y = y + yf_ref[:, gs] + dvec_ref[:, gs] * xs[:, gs]
            y = y * _silu(z_ref[:, gs].astype(F32))
            ms = jnp.mean(y * y, axis=-1, keepdims=True)
            y_ref[:, gs] = (y * lax.rsqrt(ms + EPS) * nw_ref[:, gs]).astype(y_ref.dtype)
        else:
            y_ref[:, gs] = y

    @pl.when(c == nc - 1)
    def _():
        fin_ref[...] = st_ref[...]


def ssd_scan(xbc, p_ssd, dt_raw, consts, init, n_batch, nc, blk0, reverse, y_fwd=None, y_all=None):
    width = xbc.shape[1] - 2 * SSD_GROUPS * SSD_STATE
    bc_w = SSD_GROUPS * SSD_STATE
    gw = width // SSD_GROUPS
    combine = reverse
    t_all = xbc.shape[0]

    def cidx(c):
        return (nc - 1 - c) if reverse else c

    def rows(b, c):
        return blk0 + b * nc + cidx(c)

    in_specs = [pl.BlockSpec((CHUNK, width), lambda b, c: (rows(b, c), 0)),
                pl.BlockSpec((CHUNK, bc_w), lambda b, c: (rows(b, c), width // bc_w)),
                pl.BlockSpec((CHUNK, bc_w), lambda b, c: (rows(b, c), width // bc_w + 1)),
                pl.BlockSpec((CHUNK, LANE), lambda b, c: (rows(b, c), 0)),
                pl.BlockSpec((1, LANE), lambda b, c: (0, 0)),
                pl.BlockSpec((1, LANE), lambda b, c: (0, 0)),
                pl.BlockSpec((LANE, width), lambda b, c: (0, 0)),
                pl.BlockSpec((CHUNK, CHUNK), lambda b, c: (0, 0)),
                pl.BlockSpec((None, SSD_GROUPS, SSD_STATE, gw), lambda b, c: (b, 0, 0, 0))]
    args = [xbc, xbc, xbc, dt_raw, consts["dt_bias"], consts["neg_a"],
            consts["expand_rev" if reverse else "expand_fwd"],
            consts["tri_rev" if reverse else "tri_fwd"], init]
    st_shape = jax.ShapeDtypeStruct((n_batch, SSD_GROUPS, SSD_STATE, gw), F32)
    st_spec = pl.BlockSpec((None, SSD_GROUPS, SSD_STATE, gw), lambda b, c: (b, 0, 0, 0))
    aliases = {}
    if combine:
        in_specs += [pl.BlockSpec((CHUNK, width), lambda b, c: (b * nc + cidx(c), 0)),
                     pl.BlockSpec((CHUNK, width), lambda b, c: (rows(b, c), 0)),
                     pl.BlockSpec((1, width), lambda b, c: (0, 0)),
                     pl.BlockSpec((1, width), lambda b, c: (0, 0)),
                     pl.BlockSpec(memory_space=pl.ANY)]
        if y_all is None:
            y_all = jnp.zeros((SUBLANE, LANE), BF16)
        else:
            aliases = {len(in_specs) - 1: 0}
        args += [y_fwd, p_ssd, consts["d_vec"], consts["ssd_norm"], y_all]
        y_shape = jax.ShapeDtypeStruct((t_all, width), BF16)
        y_spec = pl.BlockSpec((CHUNK, width), lambda b, c: (rows(b, c), 0))
    else:
        y_shape = jax.ShapeDtypeStruct((n_batch * nc * CHUNK, width), F32)
        y_spec = pl.BlockSpec((CHUNK, width), lambda b, c: (b * nc + c, 0))
    kern = functools.partial(_ssd_kernel, reverse=reverse, combine=combine, nc=nc,
                             dcol=(width // SSD_HEAD_DIM) if reverse else 0)
    return pl.pallas_call(
        kern, grid=(n_batch, nc), in_specs=in_specs,
        out_specs=[y_spec, st_spec], out_shape=[y_shape, st_shape],
        scratch_shapes=[pltpu.VMEM((SSD_GROUPS, SSD_STATE, gw), F32)],
        input_output_aliases=aliases,
        compiler_params=_cparams(2),
        name="ssd_scan_rev" if reverse else "ssd_scan_fwd")(*args)


def _hgrn_level_consts(reverse):
    c = CHUNK
    idx = np.arange(c)
    if reverse:
        tri = (idx[None, :] >= idx[:, None]).astype(np.float32)
    else:
        tri = (idx[None, :] <= idx[:, None]).astype(np.float32)
    levels = [c >> (k + 1) for k in range(int(math.log2(c)))]
    mats = [tri]
    qa = []
    for s in levels:
        base = idx & ~(2 * s - 1)
        m = base + s if reverse else base + s - 1
        mats.append(tri - tri[m])
        q_side = (idx & s) == 0 if reverse else (idx & s) != 0
        qa.append(np.broadcast_to(q_side[:, None].astype(np.float32), (c, LANE)))
    lv = np.full((c, c), -1.0, np.float32)
    for i in range(c):
        for j in range(c):
            if i == j:
                lv[i, j] = len(levels)
            elif (j > i) == reverse:
                s = 1 << int(math.floor(math.log2(i ^ j)))
                lv[i, j] = levels.index(s)
    return (jnp.asarray(np.concatenate(mats, axis=0), BF16), jnp.asarray(np.stack(qa), F32),
            jnp.asarray(lv, F32), len(levels))


def _hgrn_kernel(*refs, reverse, combine, nc, n_lev):
    if combine:
        (q_ref, f_ref, v_ref, llb_ref, l1m_ref, ld_ref, qa_ref, lv_ref, init_ref,
         of_ref, g_ref, nw_ref, _, o_ref, fin_ref, st_ref) = refs
    else:
        (q_ref, f_ref, v_ref, llb_ref, l1m_ref, ld_ref, qa_ref, lv_ref, init_ref,
         o_ref, fin_ref, st_ref) = refs
    c = pl.program_id(1)

    @pl.when(c == 0)
    def _():
        st_ref[...] = init_ref[...]

    n_heads = st_ref.shape[0]
    ld = ld_ref[...]
    lv = lv_ref[...]
    edge = 0 if reverse else CHUNK - 1
    for h in range(n_heads):
        hs = slice(h * HG_K, (h + 1) * HG_K)
        fr = f_ref[:, hs].astype(F32)
        sp = jnp.log1p(jnp.exp(-jnp.abs(fr)))
        log_sig = jnp.minimum(fr, 0.0) - sp
        log_sig_neg = jnp.minimum(-fr, 0.0) - sp
        a = llb_ref[:, hs]
        l1m = l1m_ref[:, hs]
        b = l1m + log_sig
        log_f = jnp.maximum(a, b) + jnp.log1p(jnp.exp(-jnp.abs(a - b)))
        kk = jnp.exp(l1m + log_sig_neg)
        allc = _dot_exact_lhs(ld, log_f)
        cum = allc[0:CHUNK]
        last = cum[edge:edge + 1]
        qh = _silu(q_ref[:, hs].astype(F32))
        vh = v_ref[:, hs]
        st = st_ref[h]
        o = _dot_nt((qh * jnp.exp(cum)).astype(BF16), st.astype(BF16))
        kd = (kk * jnp.exp(last - cum)).astype(BF16)
        st_ref[h] = st * jnp.exp(last) + _dot_tn(vh, kd)
        sc = jnp.where(lv == n_lev, _dot_nt(qh.astype(BF16), kk.astype(BF16)), 0.0)
        for s in range(n_lev):
            e = jnp.exp(-jnp.abs(allc[(s + 1) * CHUNK:(s + 2) * CHUNK]))
            zz = (jnp.where(qa_ref[s] > 0.5, qh, kk) * e).astype(BF16)
            sc = jnp.where(lv == s, _dot_nt(zz, zz), sc)
        o = o + _dot(sc.astype(BF16), vh)
        if combine:
            o = o + of_ref[:, hs]
            ms = jnp.mean(o * o, axis=-1, keepdims=True)
            o = o * lax.rsqrt(ms + EPS) * nw_ref[...]
            o_ref[:, hs] = (o * _silu(g_ref[:, hs].astype(F32))).astype(o_ref.dtype)
        else:
            o_ref[:, hs] = o

    @pl.when(c == nc - 1)
    def _():
        fin_ref[...] = st_ref[...]


def hgrn_scan(p_hg, consts, init, n_batch, nc, blk0, rows_per_col, reverse, o_fwd=None, o_all=None):
    t_all, n_cols = p_hg.shape
    width = n_cols // 5
    n_heads = width // HG_K
    combine = reverse
    seg_f = 2 if reverse else 1
    if rows_per_col:
        nrb = rows_per_col // CHUNK
        vshape = lambda r, w: (r // GRID_W, GRID_W * w)
        view = lambda a: a.reshape(vshape(*a.shape))

        def place(b, k, seg, nseg, base):
            return (base + b * nrb + k % nrb, (k // nrb) * nseg + seg)
    else:
        vshape = lambda r, w: (r, w)
        view = lambda a: a

        def place(b, k, seg, nseg, base):
            return (base + b * nc + k, seg)

    def cidx(c):
        return (nc - 1 - c) if reverse else c

    def in_spec(seg):
        return pl.BlockSpec((CHUNK, width), lambda b, c: place(b, cidx(c), seg, 5, blk0))

    ld, qa, lv, n_lev = _hgrn_level_consts(reverse)
    pv = view(p_hg)
    in_specs = [in_spec(0), in_spec(seg_f), in_spec(3),
                pl.BlockSpec((1, width), lambda b, c: (0, 0)),
                pl.BlockSpec((1, width), lambda b, c: (0, 0)),
                pl.BlockSpec(ld.shape, lambda b, c: (0, 0)),
                pl.BlockSpec(qa.shape, lambda b, c: (0, 0, 0)),
                pl.BlockSpec(lv.shape, lambda b, c: (0, 0)),
                pl.BlockSpec((None, n_heads, HG_V, HG_K), lambda b, c: (b, 0, 0, 0))]
    d = 1 if reverse else 0
    args = [pv, pv, pv, consts["log_lb"][d:d + 1], consts["log_1m_lb"][d:d + 1], ld, qa, lv, init]
    st_shape = jax.ShapeDtypeStruct((n_batch, n_heads, HG_V, HG_K), F32)
    st_spec = pl.BlockSpec((None, n_heads, HG_V, HG_K), lambda b, c: (b, 0, 0, 0))
    aliases = {}
    if combine:
        in_specs += [pl.BlockSpec((CHUNK, width), lambda b, c: place(b, cidx(c), 0, 1, 0)),
                     in_spec(4),
                     pl.BlockSpec((1, HG_V), lambda b, c: (0, 0)),
                     pl.BlockSpec(memory_space=pl.ANY)]
        if o_all is None:
            o_all = jnp.zeros((SUBLANE, LANE), BF16)
        else:
            o_all = view(o_all)
            aliases = {len(in_specs) - 1: 0}
        args += [view(o_fwd), pv, consts["hg_norm"], o_all]
        o_shape = jax.ShapeDtypeStruct(vshape(t_all, width), BF16)
        o_spec = pl.BlockSpec((CHUNK, width), lambda b, c: place(b, cidx(c), 0, 1, blk0))
    else:
        o_shape = jax.ShapeDtypeStruct(vshape(n_batch * nc * CHUNK, width), F32)
        o_spec = pl.BlockSpec((CHUNK, width), lambda b, c: place(b, cidx(c), 0, 1, 0))
    kern = functools.partial(_hgrn_kernel, reverse=reverse, combine=combine, nc=nc, n_lev=n_lev)
    out, fin = pl.pallas_call(
        kern, grid=(n_batch, nc), in_specs=in_specs,
        out_specs=[o_spec, st_spec], out_shape=[o_shape, st_shape],
        scratch_shapes=[pltpu.VMEM((n_heads, HG_V, HG_K), F32)],
        input_output_aliases=aliases,
        compiler_params=_cparams(2),
        name="hgrn_scan_rev" if reverse else "hgrn_scan_fwd")(*args)
    return out.reshape(-1, width), fin


def _final_norm_kernel(x_ref, w_ref, o_ref):
    x = x_ref[...]
    ms = jnp.mean(x * x, axis=-1, keepdims=True)
    o_ref[...] = x * lax.rsqrt(ms + EPS) * w_ref[...]


def final_norm(x_all, w, t_out, tm):
    d = x_all.shape[1]
    return pl.pallas_call(
        _final_norm_kernel, grid=(t_out // tm,),
        in_specs=[pl.BlockSpec((tm, d), lambda i: (i, 0)), pl.BlockSpec((1, d), lambda i: (0, 0))],
        out_specs=pl.BlockSpec((tm, d), lambda i: (i, 0)),
        out_shape=jax.ShapeDtypeStruct((t_out, d), F32),
        compiler_params=_cparams(1), name="final_norm")(x_all, w.reshape(1, d))


def _pad_cols(a, n):
    return jnp.pad(a, ((0, 0), (0, n - a.shape[1])))


def _expand_matrix(first_col, n_heads, head_dim):
    e = np.zeros((LANE, n_heads * head_dim), np.float32)
    for h in range(n_heads):
        e[first_col + h, h * head_dim:(h + 1) * head_dim] = 1.0
    return jnp.asarray(e, BF16)


def kernel(x, c, ctx, c_ctx, ada_w, ada_b, norm_mix_w, norm_ffn_w, w_in, ssd_conv_w, ssd_conv_b,
           ssd_dt_bias, ssd_a_log, ssd_d, ssd_norm_w, hgrn_lb_logits, hgrn_norm_w, w_out, router_w,
           moe_w_gate, moe_w_up, moe_w_down, final_norm_w):
    n_batch, seq, d = x.shape
    ctx_len = ctx.shape[1]
    depth = ada_w.shape[0]
    t_lat, t_ctx = n_batch * seq, n_batch * ctx_len
    t_all = t_lat + t_ctx
    ssd_w = d // 2
    hg_w = d // 2
    ssd_heads = ssd_w // SSD_HEAD_DIM
    conv_ch = ssd_w + 2 * SSD_GROUPS * SSD_STATE
    n_exp = router_w.shape[1]
    d_exp = moe_w_gate.shape[-1]
    rows = seq // GRID_W
    assert rows % CHUNK == 0 and ctx_len % CHUNK == 0 and 2 * ssd_heads <= LANE

    tm = math.gcd(512, math.gcd(seq, t_ctx))
    tl = math.gcd(256, ctx_len)
    tn = 1024

    idx = np.arange(CHUNK)
    consts_static = {
        "tri_fwd": jnp.asarray(idx[None, :] <= idx[:, None], BF16),
        "tri_rev": jnp.asarray(idx[None, :] >= idx[:, None], BF16),
        "expand_fwd": _expand_matrix(0, ssd_heads, SSD_HEAD_DIM),
        "expand_rev": _expand_matrix(ssd_heads, ssd_heads, SSD_HEAD_DIM),
    }
    gate_expand = _expand_matrix(0, n_exp, d_exp)

    lb_p = jax.nn.softmax(hgrn_lb_logits.astype(F32), axis=0)
    lower = jnp.cumsum(lb_p, axis=0) - lb_p[0]
    log_lb = jnp.log(lower)
    log_1m_lb = jnp.log1p(-lower)

    cc = jnp.concatenate([c, c_ctx[None], jnp.zeros((SUBLANE - n_batch - 1, d), F32)], axis=0)
    mod_all = ada_modulation(cc, ada_w, ada_b)

    r_pad = _pad_cols(router_w.astype(F32), LANE)
    r_hi = r_pad.astype(BF16)
    r_lo = (r_pad - r_hi.astype(F32)).astype(BF16)

    x_all = jnp.concatenate([x.reshape(t_lat, d), ctx.reshape(t_ctx, d)], axis=0)

    lat_nc, ctx_nc = seq // CHUNK, ctx_len // CHUNK
    ctx_blk0 = t_lat // CHUNK
    zero_ssd = jnp.zeros((n_batch, SSD_GROUPS, SSD_STATE, ssd_w // SSD_GROUPS), F32)
    zero_hg = jnp.zeros((n_batch, hg_w // HG_K, HG_V, HG_K), F32)

    for l in range(depth):
        mod = mod_all[l].reshape(SUBLANE * N_MOD, 1, d)
        wl = w_in[l]
        w_ssd = wl[:, :ssd_w + conv_ch].astype(BF16)
        w_dt = _pad_cols(wl[:, ssd_w + conv_ch:ssd_w + conv_ch + 2 * ssd_heads], LANE).astype(BF16)
        w_hg = wl[:, ssd_w + conv_ch + 2 * ssd_heads:].astype(BF16)
        consts = dict(consts_static)
        consts["dt_bias"] = _pad_cols(ssd_dt_bias[l].reshape(1, -1).astype(F32), LANE)
        consts["neg_a"] = _pad_cols(-jnp.exp(ssd_a_log[l].reshape(1, -1).astype(F32)), LANE)
        consts["d_vec"] = jnp.repeat(ssd_d[l].astype(F32), SSD_HEAD_DIM).reshape(1, ssd_w)
        consts["ssd_norm"] = ssd_norm_w[l].astype(F32).reshape(1, ssd_w)
        consts["log_lb"] = log_lb[l]
        consts["log_1m_lb"] = log_1m_lb[l]
        consts["hg_norm"] = hgrn_norm_w[l].astype(F32).reshape(1, HG_V)

        h = norm_modulate(x_all, norm_mix_w[l], mod, 0, 1, seq, n_batch, tm)
        p_ssd = matmul(h, w_ssd, BF16, tm, tn)
        p_hg = matmul(h, w_hg, BF16, tm, tn)
        dt_raw = matmul(h, w_dt, F32, tm, LANE)
        xbc = conv_silu(p_ssd, ssd_w, ssd_conv_w[l].astype(F32), ssd_conv_b[l].astype(F32),
                        seq, ctx_len, t_lat, tl, 1024)

        yf_c, sf_c = ssd_scan(xbc, p_ssd, dt_raw, consts, zero_ssd, n_batch, ctx_nc, ctx_blk0, False)
        y_ssd, sb_c = ssd_scan(xbc, p_ssd, dt_raw, consts, zero_ssd, n_batch, ctx_nc, ctx_blk0, True,
                               y_fwd=yf_c)
        yf_l, _ = ssd_scan(xbc, p_ssd, dt_raw, consts, sf_c, n_batch, lat_nc, 0, False)
        y_ssd, _ = ssd_scan(xbc, p_ssd, dt_raw, consts, sb_c, n_batch, lat_nc, 0, True,
                            y_fwd=yf_l, y_all=y_ssd)

        of_c, hf_c = hgrn_scan(p_hg, consts, zero_hg, n_batch, ctx_nc, ctx_blk0, None, False)
        y_hg, hb_c = hgrn_scan(p_hg, consts, zero_hg, n_batch, ctx_nc, ctx_blk0, None, True,
                               o_fwd=of_c)
        of_l, _ = hgrn_scan(p_hg, consts, hf_c, n_batch, lat_nc, 0, rows, False)
        y_hg, _ = hgrn_scan(p_hg, consts, hb_c, n_batch, lat_nc, 0, rows, True,
                            o_fwd=of_l, o_all=y_hg)

        x_all = matmul_residual(y_ssd, 0, y_hg, 0, w_out[l].astype(BF16), x_all, mod, 2,
                                seq, n_batch, tm, tn)

        h2, gates = norm_modulate(x_all, norm_ffn_w[l], mod, 3, 4, seq, n_batch, tm,
                                  router=(r_hi, r_lo, n_exp))
        wg = moe_w_gate[l].transpose(1, 0, 2).reshape(d, n_exp * d_exp).astype(BF16)
        wu = moe_w_up[l].transpose(1, 0, 2).reshape(d, n_exp * d_exp).astype(BF16)
        wd = moe_w_down[l].reshape(n_exp * d_exp, d).astype(BF16)
        act = glu_matmul(h2, wg, wu, gates, gate_expand, tm, tn)
        x_all = matmul_residual(act, 0, act, 1, wd, x_all, mod, 5, seq, n_batch, tm, tn)

    out = final_norm(x_all, final_norm_w.astype(F32), t_lat, tm)
    return out.reshape(n_batch, seq, d)
```
